```python
import math
import jax, jax.numpy as jnp
from jax import lax
import numpy as np

D_MODEL = 2048
BATCH = 8
SEQ = 4096
DEPTH = 4

HEAD_DIM = 64
N_HEADS_TOTAL = D_MODEL // HEAD_DIM
N_HEADS_A = N_HEADS_TOTAL // 4
N_KV_A = N_HEADS_A // 4
N_HEADS_B = N_HEADS_TOTAL // 4
N_HEADS_C = N_HEADS_TOTAL // 2
BLOCK = 128
WINDOW_A = 128
DILATED_PAIRS = ((128, 1), (512, 4), (2048, 16))
N_BUCKETS = 32
T5_MAX_DIST = 2048
D_FF = 256 * (-(-(8 * D_MODEL // 3) // 256))
CONV_WIDTH = 3
EPS = 1e-6
NEG_INF = -1e30

A_Q = N_HEADS_A * HEAD_DIM
A_KV = N_KV_A * HEAD_DIM
B_W = N_HEADS_B * HEAD_DIM
C_W = N_HEADS_C * HEAD_DIM
IN_WIDTH = A_Q + 2 * A_KV + 3 * B_W + 3 * C_W
MIX_WIDTH = A_Q + B_W + C_W

kernel_name = "hymba_style_swa_stickbreak_dilated_convffn"


def rmsnorm(x, g):
    xf = x.astype(jnp.float32)
    y = xf * lax.rsqrt(jnp.mean(xf * xf, axis=-1, keepdims=True) + EPS)
    return (y * g.astype(jnp.float32)).astype(x.dtype)


def t5_bucket(dist):
    max_exact = N_BUCKETS // 2
    d = jnp.maximum(dist, 0)
    large = max_exact + (jnp.log(jnp.maximum(d, 1).astype(jnp.float32) / max_exact)
                         / math.log(T5_MAX_DIST / max_exact) * (N_BUCKETS - max_exact)).astype(jnp.int32)
    large = jnp.minimum(large, N_BUCKETS - 1)
    return jnp.where(d < max_exact, d, large)


def block_rel_bias(table, dil):
    rel = jnp.arange(BLOCK)[:, None] + BLOCK - jnp.arange(2 * BLOCK)[None, :]
    buckets = t5_bucket(rel * dil)
    return jnp.transpose(table[buckets], (2, 0, 1)).astype(jnp.float32)


def banded_attention(q, k, v, bias, max_dist, sinks=None):
    n, length, hq, hd = q.shape
    hk = k.shape[2]
    grp = hq // hk
    lp = -(-length // BLOCK) * BLOCK
    if lp != length:
        padw = ((0, 0), (0, lp - length), (0, 0), (0, 0))
        q, k, v = jnp.pad(q, padw), jnp.pad(k, padw), jnp.pad(v, padw)
    nb = lp // BLOCK
    qb = q.reshape(n, nb, BLOCK, hk, grp, hd).astype(jnp.float32)
    kb = k.reshape(n, nb, BLOCK, hk, hd).astype(jnp.float32)
    vb = v.reshape(n, nb, BLOCK, hk, hd).astype(jnp.float32)
    prev = ((0, 0), (1, 0), (0, 0), (0, 0), (0, 0))
    kk = jnp.concatenate([jnp.pad(kb[:, :-1], prev), kb], axis=2)
    vv = jnp.concatenate([jnp.pad(vb[:, :-1], prev), vb], axis=2)
    logits = (jnp.einsum('nbqhgd,nbkhd->nbhgqk', qb, kk) * (hd ** -0.5)
              + bias.reshape(hk, grp, BLOCK, 2 * BLOCK))
    rel = jnp.arange(BLOCK)[:, None] + BLOCK - jnp.arange(2 * BLOCK)[None, :]
    key_abs = (jnp.arange(nb)[:, None] - 1) * BLOCK + jnp.arange(2 * BLOCK)[None, :]
    mask = ((rel >= 0) & (rel <= max_dist))[None] & (key_abs >= 0)[:, None, :]
    logits = jnp.where(mask[None, :, None, None], logits, NEG_INF)
    m = jnp.max(logits, axis=-1, keepdims=True)
    if sinks is not None:
        s = sinks.astype(jnp.float32).reshape(hk, grp, 1, 1)
        m = jnp.maximum(m, s)
    p = jnp.exp(logits - m)
    denom = jnp.sum(p, axis=-1, keepdims=True)
    if sinks is not None:
        denom = denom + jnp.exp(s - m)
    out = jnp.einsum('nbhgqk,nbkhd->nbhgqd', p / denom, vv)
    out = jnp.transpose(out, (0, 1, 4, 2, 3, 5)).reshape(n, lp, hq, hd)[:, :length]
    lse = jnp.transpose((m + jnp.log(denom))[..., 0], (0, 1, 4, 2, 3)).reshape(n, lp, hq)[:, :length]
    return out, lse


def stick_breaking_attention(q, k, v):
    b_, s_, h, hd = q.shape
    nb = s_ // BLOCK
    qb = jnp.transpose(q.reshape(b_, nb, BLOCK, h, hd), (1, 0, 2, 3, 4))
    kf = k.astype(jnp.float32)
    vf = v.astype(jnp.float32)
    s_pos = jnp.arange(s_)

    def one_block(args):
        qblk, blk = args
        z = jnp.einsum('bqhd,bkhd->bhqk', qblk.astype(jnp.float32), kf) * (hd ** -0.5)
        t_pos = blk * BLOCK + jnp.arange(BLOCK)
        causal = s_pos[None, :] < t_pos[:, None]
        log_rem = jnp.where(causal, jax.nn.log_sigmoid(-z), 0.0)
        suffix = lax.cumsum(log_rem, axis=3, reverse=True) - log_rem
        a = jnp.where(causal, jnp.exp(jax.nn.log_sigmoid(z) + suffix), 0.0)
        return jnp.einsum('bhqk,bkhd->bqhd', a, vf)

    out = lax.map(one_block, (qb, jnp.arange(nb)))
    return jnp.transpose(out, (1, 0, 2, 3, 4)).reshape(b_, s_, h, hd)


def dilated_attention(q, k, v, table_c):
    b_, s_, h, hd = q.shape
    outs, lses = [], []
    for window, dil in DILATED_PAIRS:
        def to_sub(t):
            return jnp.transpose(t.reshape(b_, s_ // dil, dil, h, hd), (0, 2, 1, 3, 4)).reshape(b_ * dil, s_ // dil, h, hd)
        o, lse = banded_attention(to_sub(q), to_sub(k), to_sub(v), block_rel_bias(table_c, dil), window // dil)
        outs.append(jnp.transpose(o.reshape(b_, dil, s_ // dil, h, hd), (0, 2, 1, 3, 4)).reshape(b_, s_, h, hd))
        lses.append(jnp.transpose(lse.reshape(b_, dil, s_ // dil, h), (0, 2, 1, 3)).reshape(b_, s_, h))
    w = jax.nn.softmax(jnp.stack(lses, axis=0), axis=0)
    return jnp.sum(w[..., None] * jnp.stack(outs, axis=0), axis=0)


def causal_dwconv(u, w, b):
    up = jnp.pad(u, ((0, 0), (CONV_WIDTH - 1, 0), (0, 0)))
    s_ = u.shape[1]
    acc = b
    for i in range(CONV_WIDTH):
        acc = acc + w[i] * up[:, i:i + s_]
    return acc


def setup_inputs(seed: int = 0) -> dict:
    key = jax.random.key(seed)
    ks = jax.random.split(key, 20)
    f32 = jnp.float32

    def nrm(k, shape, scale):
        return jax.random.normal(k, shape, f32) * scale

    return {
        "x": nrm(ks[0], (BATCH, SEQ, D_MODEL), 1.0),
        "attn_norm": 1.0 + nrm(ks[1], (DEPTH, D_MODEL), 0.02),
        "w_in": nrm(ks[2], (DEPTH, D_MODEL, IN_WIDTH), D_MODEL ** -0.5),
        "a_q_gain": 1.0 + nrm(ks[3], (DEPTH, HEAD_DIM), 0.02),
        "a_k_gain": 1.0 + nrm(ks[4], (DEPTH, HEAD_DIM), 0.02),
        "a_sinks": nrm(ks[5], (DEPTH, N_HEADS_A), 0.5),
        "c_q_gain": 1.0 + nrm(ks[6], (DEPTH, HEAD_DIM), 0.02),
        "c_k_gain": 1.0 + nrm(ks[7], (DEPTH, HEAD_DIM), 0.02),
        "rel_bias_table": nrm(ks[8], (N_BUCKETS, N_HEADS_A + N_HEADS_C), 0.5),
        "mix_out_gain": 1.0 + nrm(ks[9], (DEPTH, MIX_WIDTH), 0.02),
        "w_out": nrm(ks[10], (DEPTH, MIX_WIDTH, D_MODEL), MIX_WIDTH ** -0.5),
        "ffn_norm": 1.0 + nrm(ks[11], (DEPTH, D_MODEL), 0.02),
        "w_up": nrm(ks[12], (DEPTH, D_MODEL, 2 * D_FF), D_MODEL ** -0.5),
        "conv_w": nrm(ks[13], (DEPTH, CONV_WIDTH, 2 * D_FF), CONV_WIDTH ** -0.5),
        "conv_b": nrm(ks[14], (DEPTH, 2 * D_FF), 0.02),
        "w_down": nrm(ks[15], (DEPTH, D_FF, D_MODEL), D_FF ** -0.5),
    }


def reference(x, attn_norm, w_in, a_q_gain, a_k_gain, a_sinks, c_q_gain, c_k_gain, rel_bias_table,
              mix_out_gain, w_out, ffn_norm, w_up, conv_w, conv_b, w_down):
    b_, s_, _ = x.shape
    sizes = [A_Q, A_KV, A_KV, B_W, B_W, B_W, C_W, C_W, C_W]
    offsets = [int(o) for o in np.cumsum(sizes)[:-1]]
    table_a = rel_bias_table[:, :N_HEADS_A]
    table_c = rel_bias_table[:, N_HEADS_A:]
    bias_a = block_rel_bias(table_a, 1)
    for l in range(DEPTH):
        h = rmsnorm(x, attn_norm[l])
        proj = jnp.einsum('bsd,de->bse', h, w_in[l])
        aq, ak, av, bq, bk, bv, cq, ck, cv = jnp.split(proj, offsets, axis=-1)
        heads = lambda t, n: t.reshape(b_, s_, n, HEAD_DIM)
        out_a, _ = banded_attention(rmsnorm(heads(aq, N_HEADS_A), a_q_gain[l]),
                                    rmsnorm(heads(ak, N_KV_A), a_k_gain[l]),
                                    heads(av, N_KV_A), bias_a, WINDOW_A - 1, sinks=a_sinks[l])
        out_b = stick_breaking_attention(heads(bq, N_HEADS_B), heads(bk, N_HEADS_B), heads(bv, N_HEADS_B))
        out_c = dilated_attention(rmsnorm(heads(cq, N_HEADS_C), c_q_gain[l]),
                                  rmsnorm(heads(ck, N_HEADS_C), c_k_gain[l]),
                                  heads(cv, N_HEADS_C), table_c)
        g = mix_out_gain[l]
        ya = rmsnorm(out_a.reshape(b_, s_, A_Q), g[:A_Q])
        yb = rmsnorm(out_b.reshape(b_, s_, B_W), g[A_Q:A_Q + B_W])
        yc = rmsnorm(out_c.reshape(b_, s_, C_W), g[A_Q + B_W:])
        mix = jnp.concatenate([ya, yb, yc], axis=-1).astype(x.dtype)
        x = x + jnp.einsum('bse,ed->bsd', mix, w_out[l])
        h = rmsnorm(x, ffn_norm[l])
        u = causal_dwconv(jnp.einsum('bsd,df->bsf', h, w_up[l]), conv_w[l], conv_b[l])
        gate, up = jnp.split(u, [D_FF], axis=-1)
        x = x + jnp.einsum('bsf,fd->bsd', jax.nn.silu(gate) * up, w_down[l])
    return x
```

```python
import functools
import math

import jax
import jax.numpy as jnp
import numpy as np
from jax import lax
from jax.experimental import pallas as pl
from jax.experimental.pallas import tpu as pltpu

F32 = jnp.float32
BF16 = jnp.bfloat16

D_MODEL = 2048
HEAD_DIM = 64
PAIR = 2 * HEAD_DIM
N_HEADS_A = 8
N_KV_A = 2
N_HEADS_B = 8
N_HEADS_C = 16
BLOCK = 128
WINDOW_A = 128
DILATED_PAIRS = ((128, 1), (512, 4), (2048, 16))
N_BUCKETS = 32
T5_MAX_DIST = 2048
D_FF = 5632
EPS = 1e-6
NEG_INF = -1e30

A_Q = N_HEADS_A * HEAD_DIM
A_KV = N_KV_A * HEAD_DIM
B_W = N_HEADS_B * HEAD_DIM
C_W = N_HEADS_C * HEAD_DIM
IN_WIDTH = A_Q + 2 * A_KV + 3 * B_W + 3 * C_W

COL_AQ = 0
COL_AK = A_Q // PAIR
COL_AV = COL_AK + A_KV // PAIR
COL_BQ = COL_AV + A_KV // PAIR
COL_BK = COL_BQ + B_W // PAIR
COL_BV = COL_BK + B_W // PAIR
COL_CQ = COL_BV + B_W // PAIR
COL_CK = COL_CQ + C_W // PAIR
COL_CV = COL_CK + C_W // PAIR

VMEM_LIMIT = 56 * 1024 * 1024

ROW_TILE_IN = 1024
COL_TILE_IN = 768
ROW_TILE_OUT = 512
ROW_TILE_FFN = 512
FF_CHUNK = 512
HALO = 16
SB_QROWS = 256
SB_KEYS = 256
PRE_CHUNK = 512


def _params(semantics):
    return pltpu.CompilerParams(dimension_semantics=semantics, vmem_limit_bytes=VMEM_LIMIT)


def _rms_rows(x, g):
    ms = jnp.mean(x * x, axis=-1, keepdims=True)
    return x * lax.rsqrt(ms + EPS) * g


def _lane_is_first_head(rows=1):
    return lax.broadcasted_iota(jnp.int32, (rows, PAIR), 1) < HEAD_DIM


def _pair_rmsnorm(x, g2):
    first = _lane_is_first_head()
    sq = x * x
    s_first = jnp.sum(jnp.where(first, sq, 0.0), axis=-1, keepdims=True)
    s_second = jnp.sum(jnp.where(first, 0.0, sq), axis=-1, keepdims=True)
    ms = jnp.where(first, s_first, s_second) * (1.0 / HEAD_DIM)
    return x * lax.rsqrt(ms + EPS) * g2


def _stack_heads(qb):
    first = _lane_is_first_head()
    return jnp.concatenate([jnp.where(first, qb, 0.0), jnp.where(first, 0.0, qb)], axis=0)


def _unstack_heads(o2):
    rows = o2.shape[0] // 2
    return jnp.where(_lane_is_first_head(), o2[:rows], o2[rows:])


def _dot_nt(a, b):
    return lax.dot_general(a, b, (((1,), (1,)), ((), ())), preferred_element_type=F32)


def _dot(a, b):
    return jnp.dot(a, b, preferred_element_type=F32)


def _in_proj_kernel(x_ref, g_ref, w_ref, o_ref, h_ref):
    @pl.when(pl.program_id(1) == 0)
    def _():
        h_ref[...] = _rms_rows(x_ref[...], g_ref[...]).astype(BF16)

    o_ref[...] = _dot(h_ref[...], w_ref[...])


def _in_proj(x2, g, w, layer):
    m, d = x2.shape
    n = w.shape[-1]
    tm, tn = ROW_TILE_IN, COL_TILE_IN
    return pl.pallas_call(
        _in_proj_kernel,
        grid=(m // tm, n // tn),
        in_specs=[
            pl.BlockSpec((tm, d), lambda i, j: (i, 0)),
            pl.BlockSpec((None, 1, d), lambda i, j: (layer, 0, 0)),
            pl.BlockSpec((None, d, tn), lambda i, j: (layer, 0, j)),
        ],
        out_specs=pl.BlockSpec((tm, tn), lambda i, j: (i, j)),
        out_shape=jax.ShapeDtypeStruct((m, n), F32),
        scratch_shapes=[pltpu.VMEM((tm, d), BF16)],
        compiler_params=_params(("arbitrary", "arbitrary")),
        name="in_proj",
    )(x2, g, w)


def _attn_a_kernel(q_ref, k_ref, v_ref, gq_ref, gk_ref, sink_ref, bias_ref, o_ref, qn_ref, kn_ref, vn_ref):
    seq = q_ref.shape[0]
    kv_head = pl.program_id(1) // (N_HEADS_A // N_KV_A // 2)
    keep = jnp.logical_xor(_lane_is_first_head(), kv_head == 1)

    def prep(c, _):
        rows = pl.ds(pl.multiple_of(c * PRE_CHUNK, PRE_CHUNK), PRE_CHUNK)
        qn_ref[rows, :] = _pair_rmsnorm(q_ref[rows, :], gq_ref[...]) * (HEAD_DIM ** -0.5)
        kn = _pair_rmsnorm(k_ref[rows, :], gk_ref[...])
        kn_ref[rows, :] = jnp.where(keep, kn, pltpu.roll(kn, HEAD_DIM, 1)).astype(BF16)
        v = v_ref[rows, :]
        vn_ref[rows, :] = jnp.where(keep, v, pltpu.roll(v, HEAD_DIM, 1)).astype(BF16)
        return 0

    lax.fori_loop(0, seq // PRE_CHUNK, prep, 0)
    sink = sink_ref[...]

    def block(r0, kk, vv, bias):
        q2 = _stack_heads(qn_ref[pl.ds(r0, BLOCK), :]).astype(BF16)
        s = _dot_nt(q2, kk) + bias
        m = jnp.maximum(jnp.max(s, axis=-1, keepdims=True), sink)
        p = jnp.exp(s - m)
        den = jnp.sum(p, axis=-1, keepdims=True) + jnp.exp(sink - m)
        o2 = _dot(p.astype(BF16), vv) / den
        o_ref[pl.ds(r0, BLOCK), :] = _unstack_heads(o2)

    block(0, kn_ref[0:BLOCK, :], vn_ref[0:BLOCK, :], bias_ref[:, BLOCK:])

    def body(i, _):
        r0 = pl.multiple_of(i * BLOCK, BLOCK)
        keys = pl.ds(r0 - BLOCK, 2 * BLOCK)
        block(r0, kn_ref[keys, :], vn_ref[keys, :], bias_ref[...])
        return 0

    lax.fori_loop(1, seq // BLOCK, body, 0)


def _attn_a(proj, gq2, gk2, sinks, bias, layer):
    b, s, _ = proj.shape
    n_pairs = N_HEADS_A // 2
    col = lambda off: pl.BlockSpec((None, s, PAIR), lambda bi, p: (bi, 0, off + p))
    fixed = lambda off: pl.BlockSpec((None, s, PAIR), lambda bi, p: (bi, 0, off))
    return pl.pallas_call(
        _attn_a_kernel,
        grid=(b, n_pairs),
        in_specs=[
            col(COL_AQ), fixed(COL_AK), fixed(COL_AV),
            pl.BlockSpec((None, 1, PAIR), lambda bi, p: (layer, 0, 0)),
            pl.BlockSpec((None, 1, PAIR), lambda bi, p: (layer, 0, 0)),
            pl.BlockSpec((None, None, 2 * BLOCK, 1), lambda bi, p: (layer, p, 0, 0)),
            pl.BlockSpec((None, 2 * BLOCK, 2 * BLOCK), lambda bi, p: (p, 0, 0)),
        ],
        out_specs=pl.BlockSpec((None, s, PAIR), lambda bi, p: (bi, 0, p)),
        out_shape=jax.ShapeDtypeStruct((b, s, A_Q), F32),
        scratch_shapes=[pltpu.VMEM((s, PAIR), F32), pltpu.VMEM((s, PAIR), BF16), pltpu.VMEM((s, PAIR), BF16)],
        compiler_params=_params(("arbitrary", "arbitrary")),
        name="attn_swa",
    )(proj, proj, proj, gq2, gk2, sinks, bias)


def _softplus(z):
    return jnp.maximum(z, 0.0) + jnp.log(1.0 + jnp.exp(-jnp.abs(z)))


def _attn_b_kernel(q_ref, k_ref, v_ref, o_ref, kb_ref, vb_ref, acc_ref, run_ref):
    seq = q_ref.shape[0]
    qr, kt = SB_QROWS, SB_KEYS

    def prep(c, _):
        rows = pl.ds(pl.multiple_of(c * PRE_CHUNK, PRE_CHUNK), PRE_CHUNK)
        kb_ref[rows, :] = k_ref[rows, :].astype(BF16)
        vb_ref[rows, :] = v_ref[rows, :].astype(BF16)
        return 0

    lax.fori_loop(0, seq // PRE_CHUNK, prep, 0)

    tri = (lax.broadcasted_iota(jnp.int32, (kt, kt), 0) >= lax.broadcasted_iota(jnp.int32, (kt, kt), 1)).astype(BF16)

    def q_block(i, _):
        r0 = pl.multiple_of(i * qr, qr)
        q2 = _stack_heads(q_ref[pl.ds(r0, qr), :] * (HEAD_DIM ** -0.5)).astype(BF16)

        def tile(j, masked):
            k0 = pl.multiple_of(j * kt, kt)
            z = _dot_nt(q2, kb_ref[pl.ds(k0, kt), :])
            sp = _softplus(z)
            if masked:
                t_local = lax.broadcasted_iota(jnp.int32, (2 * qr, kt), 0) & (qr - 1)
                causal = lax.broadcasted_iota(jnp.int32, (2 * qr, kt), 1) < t_local
                sp = jnp.where(causal, sp, 0.0)
            hi = sp.astype(BF16)
            lo = (sp - hi.astype(F32)).astype(BF16)
            cs = _dot(hi, tri) + _dot(lo, tri) + run_ref[...]
            a = jnp.exp(z - cs)
            if masked:
                a = jnp.where(causal, a, 0.0)
            acc_ref[...] += _dot(a.astype(BF16), vb_ref[pl.ds(k0, kt), :])
            run_ref[...] = jnp.broadcast_to(cs[:, 0:1], run_ref.shape)

        acc_ref[...] = jnp.zeros_like(acc_ref)
        run_ref[...] = jnp.zeros_like(run_ref)
        tile(i, True)

        def earlier(n, _):
            tile(i - 1 - n, False)
            return 0

        lax.fori_loop(0, i, earlier, 0)
        o_ref[pl.ds(r0, qr), :] = _unstack_heads(acc_ref[...])
        return 0

    lax.fori_loop(0, seq // qr, q_block, 0)


def _attn_b(proj):
    b, s, _ = proj.shape
    n_pairs = N_HEADS_B // 2
    col = lambda off: pl.BlockSpec((None, s, PAIR), lambda bi, p: (bi, 0, off + p))
    return pl.pallas_call(
        _attn_b_kernel,
        grid=(b, n_pairs),
        in_specs=[col(COL_BQ), col(COL_BK), col(COL_BV)],
        out_specs=pl.BlockSpec((None, s, PAIR), lambda bi, p: (bi, 0, p)),
        out_shape=jax.ShapeDtypeStruct((b, s, B_W), F32),
        scratch_shapes=[pltpu.VMEM((s, PAIR), BF16), pltpu.VMEM((s, PAIR), BF16),
                        pltpu.VMEM((2 * SB_QROWS, PAIR), F32), pltpu.VMEM((2 * SB_QROWS, SB_KEYS), F32)],
        compiler_params=_params(("arbitrary", "arbitrary")),
        name="attn_stickbreak",
    )(proj, proj, proj)


def _attn_c_kernel(q_ref, k_ref, v_ref, gq_ref, gk_ref, bias_ref, o_ref, qn_ref, kn_ref, m_ref, l_ref, acc_ref):
    seq = q_ref.shape[0]

    def prep(c, _):
        rows = pl.ds(pl.multiple_of(c * PRE_CHUNK, PRE_CHUNK), PRE_CHUNK)
        qn_ref[rows, :] = _pair_rmsnorm(q_ref[rows, :], gq_ref[...]) * (HEAD_DIM ** -0.5)
        kn_ref[rows, :] = _pair_rmsnorm(k_ref[rows, :], gk_ref[...])
        return 0

    lax.fori_loop(0, seq // PRE_CHUNK, prep, 0)

    for branch, (_, dil) in enumerate(DILATED_PAIRS):
        n_sub = seq // (BLOCK * dil)

        def rows_at(start, dil=dil):
            if dil == 1:
                return pl.ds(start if isinstance(start, int) else pl.multiple_of(start, BLOCK), BLOCK)
            return pl.ds(start, BLOCK, stride=dil)

        def block(start, prev_start, branch=branch, rows_at=rows_at):
            rows = rows_at(start)
            q2 = _stack_heads(qn_ref[rows, :]).astype(BF16)
            if prev_start is None:
                kk = kn_ref[rows, :]
                vv = v_ref[rows, :]
                bias = bias_ref[branch, :, BLOCK:]
            else:
                prev = rows_at(prev_start)
                kk = jnp.concatenate([kn_ref[prev, :], kn_ref[rows, :]], axis=0)
                vv = jnp.concatenate([v_ref[prev, :], v_ref[rows, :]], axis=0)
                bias = bias_ref[branch]
            s = _dot_nt(q2, kk.astype(BF16)) + bias
            m2 = jnp.max(s, axis=-1, keepdims=True)
            p = jnp.exp(s - m2)
            l2 = jnp.sum(p, axis=-1, keepdims=True)
            num = _unstack_heads(_dot(p.astype(BF16), vv.astype(BF16)))
            m_new = _unstack_heads(jnp.broadcast_to(m2, (2 * BLOCK, PAIR)))
            l_new = _unstack_heads(jnp.broadcast_to(l2, (2 * BLOCK, PAIR)))
            if branch == 0:
                m_ref[rows, :] = m_new
                l_ref[rows, :] = l_new
                acc_ref[rows, :] = num
            else:
                m_old = m_ref[rows, :]
                m_tot = jnp.maximum(m_old, m_new)
                w_old = jnp.exp(m_old - m_tot)
                w_new = jnp.exp(m_new - m_tot)
                m_ref[rows, :] = m_tot
                l_ref[rows, :] = l_ref[rows, :] * w_old + l_new * w_new
                acc_ref[rows, :] = acc_ref[rows, :] * w_old + num * w_new

        def residue(r, _, dil=dil, n_sub=n_sub, block=block):
            block(r, None)

            def later(j, _):
                block(j * (BLOCK * dil) + r, (j - 1) * (BLOCK * dil) + r)
                return 0

            lax.fori_loop(1, n_sub, later, 0)
            return 0

        if dil == 1:
            residue(0, 0)
        else:
            lax.fori_loop(0, dil, residue, 0)

    def finish(c, _):
        rows = pl.ds(pl.multiple_of(c * PRE_CHUNK, PRE_CHUNK), PRE_CHUNK)
        o_ref[rows, :] = acc_ref[rows, :] / l_ref[rows, :]
        return 0

    lax.fori_loop(0, seq // PRE_CHUNK, finish, 0)


def _attn_c(proj, gq2, gk2, bias, layer):
    b, s, _ = proj.shape
    n_pairs = N_HEADS_C // 2
    n_br = len(DILATED_PAIRS)
    col = lambda off: pl.BlockSpec((None, s, PAIR), lambda bi, p: (bi, 0, off + p))
    return pl.pallas_call(
        _attn_c_kernel,
        grid=(b, n_pairs),
        in_specs=[
            col(COL_CQ), col(COL_CK), col(COL_CV),
            pl.BlockSpec((None, 1, PAIR), lambda bi, p: (layer, 0, 0)),
            pl.BlockSpec((None, 1, PAIR), lambda bi, p: (layer, 0, 0)),
            pl.BlockSpec((n_br, None, 2 * BLOCK, 2 * BLOCK), lambda bi, p: (0, p, 0, 0)),
        ],
        out_specs=pl.BlockSpec((None, s, PAIR), lambda bi, p: (bi, 0, p)),
        out_shape=jax.ShapeDtypeStruct((b, s, C_W), F32),
        scratch_shapes=[pltpu.VMEM((s, PAIR), F32) for _ in range(5)],
        compiler_params=_params(("arbitrary", "arbitrary")),
        name="attn_dilated",
    )(proj, proj, proj, gq2, gk2, bias)


def _out_proj_kernel(a_ref, b_ref, c_ref, g_ref, w_ref, x_ref, o_ref):
    ya = _rms_rows(a_ref[...], g_ref[:, 0:A_Q]).astype(BF16)
    yb = _rms_rows(b_ref[...], g_ref[:, A_Q:A_Q + B_W]).astype(BF16)
    yc = _rms_rows(c_ref[...], g_ref[:, A_Q + B_W:]).astype(BF16)
    acc = x_ref[...] + _dot(ya, w_ref[0:A_Q, :])
    acc = acc + _dot(yb, w_ref[A_Q:A_Q + B_W, :])
    o_ref[...] = acc + _dot(yc, w_ref[A_Q + B_W:, :])


def _out_proj(out_a, out_b, out_c, g, w, x2, layer):
    m, d = x2.shape
    tm = ROW_TILE_OUT
    rows = lambda width: pl.BlockSpec((tm, width), lambda i: (i, 0))
    return pl.pallas_call(
        _out_proj_kernel,
        grid=(m // tm,),
        in_specs=[
            rows(A_Q), rows(B_W), rows(C_W),
            pl.BlockSpec((None, 1, d), lambda i: (layer, 0, 0)),
            pl.BlockSpec((None, d, d), lambda i: (layer, 0, 0)),
            rows(d),
        ],
        out_specs=rows(d),
        out_shape=jax.ShapeDtypeStruct((m, d), F32),
        compiler_params=_params(("arbitrary",)),
        name="out_proj",
    )(out_a, out_b, out_c, g, w, x2)


def _ffn_kernel(x_ref, g_ref, wg_ref, wu_ref, cwg_ref, cwu_ref, cbg_ref, cbu_ref, wd_ref, o_ref, h_ref,
                *, tiles_per_seq):
    i = pl.program_id(0)
    tm = x_ref.shape[0]

    @pl.when(pl.program_id(1) == 0)
    def _():
        @pl.when(i % tiles_per_seq == 0)
        def _():
            h_ref[0:HALO, :] = jnp.zeros((HALO, h_ref.shape[1]), BF16)

        @pl.when(i % tiles_per_seq != 0)
        def _():
            h_ref[0:HALO, :] = h_ref[tm:tm + HALO, :]

        x = x_ref[...]
        h_ref[HALO:, :] = _rms_rows(x, g_ref[...]).astype(BF16)
        o_ref[...] = x

    h = h_ref[...]

    def conv_branch(w_ref, cw_ref, cb_ref):
        u = _dot(h, w_ref[...])
        cw = cw_ref[...]
        acc = cb_ref[...] + cw[0:1, :] * pltpu.roll(u, 2, 0)
        acc = acc + cw[1:2, :] * pltpu.roll(u, 1, 0)
        acc = acc + cw[2:3, :] * u
        return acc[HALO:, :]

    gate = conv_branch(wg_ref, cwg_ref, cbg_ref)
    up = conv_branch(wu_ref, cwu_ref, cbu_ref)
    act = gate * (1.0 / (1.0 + jnp.exp(-gate))) * up
    o_ref[...] += _dot(act.astype(BF16), wd_ref[...])


def _ffn(x2, g, w_up, conv_w, conv_b, w_down, layer, seq):
    m, d = x2.shape
    tm, fc = ROW_TILE_FFN, FF_CHUNK
    n_chunks = D_FF // fc
    return pl.pallas_call(
        functools.partial(_ffn_kernel, tiles_per_seq=seq // tm),
        grid=(m // tm, n_chunks),
        in_specs=[
            pl.BlockSpec((tm, d), lambda i, j: (i, 0)),
            pl.BlockSpec((None, 1, d), lambda i, j: (layer, 0, 0)),
            pl.BlockSpec((None, d, fc), lambda i, j: (layer, 0, j)),
            pl.BlockSpec((None, d, fc), lambda i, j: (layer, 0, j + n_chunks)),
            pl.BlockSpec((None, 3, fc), lambda i, j: (layer, 0, j)),
            pl.BlockSpec((None, 3, fc), lambda i, j: (layer, 0, j + n_chunks)),
            pl.BlockSpec((None, 1, fc), lambda i, j: (layer, 0, j)),
            pl.BlockSpec((None, 1, fc), lambda i, j: (layer, 0, j + n_chunks)),
            pl.BlockSpec((None, fc, d), lambda i, j: (layer, j, 0)),
        ],
        out_specs=pl.BlockSpec((tm, d), lambda i, j: (i, 0)),
        out_shape=jax.ShapeDtypeStruct((m, d), F32),
        scratch_shapes=[pltpu.VMEM((HALO + tm, d), BF16)],
        compiler_params=_params(("arbitrary", "arbitrary")),
        name="conv_ffn",
    )(x2, g, w_up, w_up, conv_w, conv_w, conv_b, conv_b, w_down)


def _t5_bucket(dist):
    max_exact = N_BUCKETS // 2
    d = jnp.maximum(dist, 0)
    large = max_exact + (jnp.log(jnp.maximum(d, 1).astype(jnp.float32) / max_exact)
                         / math.log(T5_MAX_DIST / max_exact) * (N_BUCKETS - max_exact)).astype(jnp.int32)
    large = jnp.minimum(large, N_BUCKETS - 1)
    return jnp.where(d < max_exact, d, large)


def _masked_pair_bias(table, dil, max_dist):
    rel = jnp.arange(BLOCK)[:, None] + BLOCK - jnp.arange(2 * BLOCK)[None, :]
    bias = jnp.transpose(table[_t5_bucket(rel * dil)], (2, 0, 1)).astype(F32)
    bias = jnp.where(((rel >= 0) & (rel <= max_dist))[None], bias, NEG_INF)
    return bias.reshape(table.shape[1] // 2, 2 * BLOCK, 2 * BLOCK)


def kernel(x, attn_norm, w_in, a_q_gain, a_k_gain, a_sinks, c_q_gain, c_k_gain, rel_bias_table,
           mix_out_gain, w_out, ffn_norm, w_up, conv_w, conv_b, w_down):
    b, s, d = x.shape
    depth = w_in.shape[0]
    assert d == D_MODEL and s % (BLOCK * DILATED_PAIRS[-1][1]) == 0 and s % ROW_TILE_IN == 0

    w_in_b, w_out_b, w_up_b, w_down_b = (w.astype(BF16) for w in (w_in, w_out, w_up, w_down))
    row = lambda t: t.reshape(depth, 1, -1)
    twice = lambda t: jnp.concatenate([t, t], axis=-1).reshape(depth, 1, PAIR)
    attn_norm_r, ffn_norm_r, mix_gain_r, conv_b_r = row(attn_norm), row(ffn_norm), row(mix_out_gain), row(conv_b)
    aq2, ak2, cq2, ck2 = twice(a_q_gain), twice(a_k_gain), twice(c_q_gain), twice(c_k_gain)
    sinks = jnp.repeat(a_sinks.reshape(depth, N_HEADS_A // 2, 2), BLOCK, axis=-1)[..., None]
    bias_a = _masked_pair_bias(rel_bias_table[:, :N_HEADS_A], 1, WINDOW_A - 1)
    bias_c = jnp.stack([_masked_pair_bias(rel_bias_table[:, N_HEADS_A:], dil, window // dil)
                        for window, dil in DILATED_PAIRS])

    x2 = x.reshape(b * s, d)
    for layer in range(depth):
        proj = _in_proj(x2, attn_norm_r, w_in_b, layer).reshape(b, s, IN_WIDTH)
        out_a = _attn_a(proj, aq2, ak2, sinks, bias_a, layer)
        out_b = _attn_b(proj)
        out_c = _attn_c(proj, cq2, ck2, bias_c, layer)
        x2 = _out_proj(out_a.reshape(b * s, A_Q), out_b.reshape(b * s, B_W), out_c.reshape(b * s, C_W),
                       mix_gain_r, w_out_b, x2, layer)
        x2 = _ffn(x2, ffn_norm_r, w_up_b, conv_w, conv_b_r, w_down_b, layer, s)
    return x2.reshape(b, s, d)
```

```python
import functools
import math

import jax
import jax.numpy as jnp
import numpy as np
from jax import lax
from jax.experimental import pallas as pl
from jax.experimental.pallas import tpu as pltpu

F32 = jnp.float32
BF16 = jnp.bfloat16

D_MODEL = 2048
HEAD_DIM = 64
PAIR = 2 * HEAD_DIM
N_HEADS_A = 8
N_KV_A = 2
N_HEADS_B = 8
N_HEADS_C = 16
BLOCK = 128
WINDOW_A = 128
DILATED_PAIRS = ((128, 1), (512, 4), (2048, 16))
N_BUCKETS = 32
T5_MAX_DIST = 2048
D_FF = 5632
EPS = 1e-6
NEG_INF = -1e30

A_Q = N_HEADS_A * HEAD_DIM
A_KV = N_KV_A * HEAD_DIM
B_W = N_HEADS_B * HEAD_DIM
C_W = N_HEADS_C * HEAD_DIM
IN_WIDTH = A_Q + 2 * A_KV + 3 * B_W + 3 * C_W

COL_AQ = 0
COL_AK = A_Q // PAIR
COL_AV = COL_AK + A_KV // PAIR
COL_BQ = COL_AV + A_KV // PAIR
COL_BK = COL_BQ + B_W // PAIR
COL_BV = COL_BK + B_W // PAIR
COL_CQ = COL_BV + B_W // PAIR
COL_CK = COL_CQ + C_W // PAIR
COL_CV = COL_CK + C_W // PAIR

VMEM_LIMIT = 56 * 1024 * 1024

ROW_TILE_IN = 1024
COL_TILE_IN = 768
ROW_TILE_OUT = 512
ROW_TILE_FFN = 512
FF_CHUNK = 512
HALO = 16
SB_QROWS = 512
SB_KEYS = 256
PRE_CHUNK = 512
ATTN_UNROLL = 4


def _params(semantics):
    return pltpu.CompilerParams(dimension_semantics=semantics, vmem_limit_bytes=VMEM_LIMIT)


def _rms_rows(x, g):
    ms = jnp.mean(x * x, axis=-1, keepdims=True)
    return x * lax.rsqrt(ms + EPS) * g


def _lane_is_first_head(rows=1):
    return lax.broadcasted_iota(jnp.int32, (rows, PAIR), 1) < HEAD_DIM


def _pair_rmsnorm(x, g2):
    first = _lane_is_first_head()
    sq = x * x
    s_first = jnp.sum(jnp.where(first, sq, 0.0), axis=-1, keepdims=True)
    s_second = jnp.sum(jnp.where(first, 0.0, sq), axis=-1, keepdims=True)
    ms = jnp.where(first, s_first, s_second) * (1.0 / HEAD_DIM)
    return x * lax.rsqrt(ms + EPS) * g2


def _stack_heads(qb):
    first = _lane_is_first_head()
    return jnp.concatenate([jnp.where(first, qb, 0.0), jnp.where(first, 0.0, qb)], axis=0)


def _unstack_heads(o2):
    rows = o2.shape[0] // 2
    return jnp.where(_lane_is_first_head(), o2[:rows], o2[rows:])


def _dot_nt(a, b):
    return lax.dot_general(a, b, (((1,), (1,)), ((), ())), preferred_element_type=F32)


def _dot(a, b):
    return jnp.dot(a, b, preferred_element_type=F32)


def _in_proj_kernel(x_ref, g_ref, w_ref, o_ref, h_ref):
    @pl.when(pl.program_id(1) == 0)
    def _():
        h_ref[...] = _rms_rows(x_ref[...], g_ref[...]).astype(BF16)

    o_ref[...] = _dot(h_ref[...], w_ref[...])


def _in_proj(x2, g, w, layer):
    m, d = x2.shape
    n = w.shape[-1]
    tm, tn = ROW_TILE_IN, COL_TILE_IN
    return pl.pallas_call(
        _in_proj_kernel,
        grid=(m // tm, n // tn),
        in_specs=[
            pl.BlockSpec((tm, d), lambda i, j: (i, 0)),
            pl.BlockSpec((None, 1, d), lambda i, j: (layer, 0, 0)),
            pl.BlockSpec((None, d, tn), lambda i, j: (layer, 0, j)),
        ],
        out_specs=pl.BlockSpec((tm, tn), lambda i, j: (i, j)),
        out_shape=jax.ShapeDtypeStruct((m, n), F32),
        scratch_shapes=[pltpu.VMEM((tm, d), BF16)],
        compiler_params=_params(("arbitrary", "arbitrary")),
        name="in_proj",
    )(x2, g, w)


def _attn_a_kernel(q_ref, k_ref, v_ref, gq_ref, gk_ref, sink_ref, bias_ref, o_ref, qn_ref, kn_ref, vn_ref):
    seq = q_ref.shape[0]
    kv_head = pl.program_id(1) // (N_HEADS_A // N_KV_A // 2)
    keep = jnp.logical_xor(_lane_is_first_head(), kv_head == 1)

    def prep(c, _):
        rows = pl.ds(pl.multiple_of(c * PRE_CHUNK, PRE_CHUNK), PRE_CHUNK)
        qn_ref[rows, :] = _pair_rmsnorm(q_ref[rows, :], gq_ref[...]) * (HEAD_DIM ** -0.5)
        kn = _pair_rmsnorm(k_ref[rows, :], gk_ref[...])
        kn_ref[rows, :] = jnp.where(keep, kn, pltpu.roll(kn, HEAD_DIM, 1)).astype(BF16)
        v = v_ref[rows, :]
        vn_ref[rows, 0:PAIR] = jnp.where(keep, v, pltpu.roll(v, HEAD_DIM, 1)).astype(BF16)
        vn_ref[rows, PAIR:] = jnp.ones((PRE_CHUNK, PAIR), BF16)
        return 0

    lax.fori_loop(0, seq // PRE_CHUNK, prep, 0)
    sink = sink_ref[...]

    def block(r0, kk, vv, bias):
        q2 = _stack_heads(qn_ref[pl.ds(r0, BLOCK), :]).astype(BF16)
        s = _dot_nt(q2, kk) + bias
        m = jnp.maximum(jnp.max(s, axis=-1, keepdims=True), sink)
        p = jnp.exp(s - jnp.concatenate([m] * (s.shape[1] // PAIR), axis=1))
        o_ext = _dot(p.astype(BF16), vv)
        den = o_ext[:, PAIR:] + jnp.exp(sink - m)
        o_ref[pl.ds(r0, BLOCK), :] = _unstack_heads(o_ext[:, :PAIR] / den)

    def later_block(r0):
        keys = pl.ds(r0 - BLOCK, 2 * BLOCK)
        block(r0, kn_ref[keys, :], vn_ref[keys, :], bias_ref[...])

    block(0, kn_ref[0:BLOCK, :], vn_ref[0:BLOCK, :], bias_ref[:, BLOCK:])
    for u in range(1, ATTN_UNROLL):
        later_block(u * BLOCK)

    def body(g, _):
        for u in range(ATTN_UNROLL):
            later_block(pl.multiple_of((g * ATTN_UNROLL + u) * BLOCK, BLOCK))
        return 0

    lax.fori_loop(1, seq // (BLOCK * ATTN_UNROLL), body, 0)


def _attn_a(proj, gq2, gk2, sinks, bias, layer):
    b, s, _ = proj.shape
    n_pairs = N_HEADS_A // 2
    col = lambda off: pl.BlockSpec((None, s, PAIR), lambda bi, p: (bi, 0, off + p))
    fixed = lambda off: pl.BlockSpec((None, s, PAIR), lambda bi, p: (bi, 0, off))
    return pl.pallas_call(
        _attn_a_kernel,
        grid=(b, n_pairs),
        in_specs=[
            col(COL_AQ), fixed(COL_AK), fixed(COL_AV),
            pl.BlockSpec((None, 1, PAIR), lambda bi, p: (layer, 0, 0)),
            pl.BlockSpec((None, 1, PAIR), lambda bi, p: (layer, 0, 0)),
            pl.BlockSpec((None, None, 2 * BLOCK, PAIR), lambda bi, p: (layer, p, 0, 0)),
            pl.BlockSpec((None, 2 * BLOCK, 2 * BLOCK), lambda bi, p: (p, 0, 0)),
        ],
        out_specs=pl.BlockSpec((None, s, PAIR), lambda bi, p: (bi, 0, p)),
        out_shape=jax.ShapeDtypeStruct((b, s, A_Q), F32),
        scratch_shapes=[pltpu.VMEM((s, PAIR), F32), pltpu.VMEM((s, PAIR), BF16), pltpu.VMEM((s, 2 * PAIR), BF16)],
        compiler_params=_params(("arbitrary", "arbitrary")),
        name="attn_swa",
    )(proj, proj, proj, gq2, gk2, sinks, bias)


LOG2_E = 1.0 / math.log(2.0)


SB_LOGIT_MAX = 126.0


def _softplus2(y):
    return jnp.log2(1.0 + jnp.exp2(y))


def _attn_b_kernel(q_ref, k_ref, v_ref, o_ref, q2_ref, kb_ref, vb_ref, mask_ref, z_ref, zm_ref, cum_ref, acc_ref,
                   run_ref):
    seq = q_ref.shape[0]
    qr, kt = SB_QROWS, SB_KEYS
    nq = seq // qr
    per_q = qr // kt
    n_tiles = per_q * nq * (nq + 1) // 2

    def prep(c, _):
        rows = pl.ds(pl.multiple_of(c * qr, qr), qr)
        kb_ref[rows, :] = k_ref[rows, :].astype(BF16)
        vb_ref[rows, :] = v_ref[rows, :].astype(BF16)
        q2_ref[c] = _stack_heads(q_ref[rows, :] * (LOG2_E * HEAD_DIM ** -0.5)).astype(BF16)
        return 0

    lax.fori_loop(0, nq, prep, 0)

    t_local = lax.broadcasted_iota(jnp.int32, (2 * qr, kt), 0) & (qr - 1)
    s_local = lax.broadcasted_iota(jnp.int32, (2 * qr, kt), 1)
    mask_ref[0] = jnp.zeros((2 * qr, kt), F32)
    for d in range(per_q):
        mask_ref[1 + d] = jnp.where(s_local + d * kt < t_local, 0.0, NEG_INF)

    tri = (lax.broadcasted_iota(jnp.int32, (kt, kt), 0) >= lax.broadcasted_iota(jnp.int32, (kt, kt), 1)).astype(BF16)
    tri2 = jnp.concatenate([tri, tri], axis=0)

    def keys_of(j):
        return pl.ds(pl.multiple_of(j * kt, kt), kt)

    def following(i, j):
        wrap = j == 0
        i_next = jnp.minimum(jnp.where(wrap, i + 1, i), nq - 1)
        return i_next, jnp.where(wrap, (i_next + 1) * per_q - 1, j - 1)

    def logits(i, j):
        z_ref[...] = _dot_nt(q2_ref[i], kb_ref[keys_of(j), :])

    def scores(i, j):
        which_mask = jnp.maximum(j - i * per_q + 1, 0)
        zm = jnp.minimum(z_ref[...] + mask_ref[which_mask], SB_LOGIT_MAX)
        sp = _softplus2(zm)
        cum_ref[...] = _dot(sp.astype(BF16), tri)
        zm_ref[...] = zm

    def weights(i, j):
        cs = cum_ref[...] + run_ref[...]
        a = jnp.exp2(zm_ref[...] - cs)
        acc_ref[...] += _dot(a.astype(BF16), vb_ref[keys_of(j), :])
        run_ref[...] = jnp.broadcast_to(cs[:, 0:1], run_ref.shape)

    acc_ref[...] = jnp.zeros_like(acc_ref)
    run_ref[...] = jnp.zeros_like(run_ref)
    first = (jnp.int32(0), jnp.int32(per_q - 1))
    logits(*first)
    scores(*first)
    logits(*following(*first))

    def step(n, tile):
        i, j = tile
        i1, j1 = following(i, j)
        i2, j2 = following(i1, j1)
        weights(i, j)
        scores(i1, j1)
        logits(i2, j2)

        @pl.when(j == 0)
        def _():
            o_ref[pl.ds(pl.multiple_of(i * qr, qr), qr), :] = _unstack_heads(acc_ref[...])
            acc_ref[...] = jnp.zeros_like(acc_ref)
            run_ref[...] = jnp.zeros_like(run_ref)

        return i1, j1

    lax.fori_loop(0, n_tiles, step, first)


def _attn_b(proj):
    b, s, _ = proj.shape
    n_pairs = N_HEADS_B // 2
    tile = (2 * SB_QROWS, SB_KEYS)
    col = lambda off: pl.BlockSpec((None, s, PAIR), lambda bi, p: (bi, 0, off + p))
    return pl.pallas_call(
        _attn_b_kernel,
        grid=(b, n_pairs),
        in_specs=[col(COL_BQ), col(COL_BK), col(COL_BV)],
        out_specs=pl.BlockSpec((None, s, PAIR), lambda bi, p: (bi, 0, p)),
        out_shape=jax.ShapeDtypeStruct((b, s, B_W), F32),
        scratch_shapes=[pltpu.VMEM((s // SB_QROWS, 2 * SB_QROWS, PAIR), BF16),
                        pltpu.VMEM((s, PAIR), BF16), pltpu.VMEM((s, PAIR), BF16),
                        pltpu.VMEM((1 + SB_QROWS // SB_KEYS,) + tile, F32),
                        pltpu.VMEM(tile, F32), pltpu.VMEM(tile, F32), pltpu.VMEM(tile, F32),
                        pltpu.VMEM((2 * SB_QROWS, PAIR), F32), pltpu.VMEM(tile, F32)],
        compiler_params=_params(("arbitrary", "arbitrary")),
        name="attn_stickbreak",
    )(proj, proj, proj)


def _attn_c_kernel(q_ref, k_ref, v_ref, gq_ref, gk_ref, bias_ref, o_ref, qn_ref, kn_ref, m_ref, l_ref, acc_ref):
    seq = q_ref.shape[0]

    def prep(c, _):
        rows = pl.ds(pl.multiple_of(c * PRE_CHUNK, PRE_CHUNK), PRE_CHUNK)
        qn_ref[rows, :] = _pair_rmsnorm(q_ref[rows, :], gq_ref[...]) * (HEAD_DIM ** -0.5)
        kn_ref[rows, :] = _pair_rmsnorm(k_ref[rows, :], gk_ref[...])
        return 0

    lax.fori_loop(0, seq // PRE_CHUNK, prep, 0)

    for branch, (_, dil) in enumerate(DILATED_PAIRS):
        n_sub = seq // (BLOCK * dil)

        def rows_at(start, dil=dil):
            if dil == 1:
                return pl.ds(start if isinstance(start, int) else pl.multiple_of(start, BLOCK), BLOCK)
            return pl.ds(start, BLOCK, stride=dil)

        def block(start, prev_start, branch=branch, rows_at=rows_at):
            rows = rows_at(start)
            q2 = _stack_heads(qn_ref[rows, :]).astype(BF16)
            if prev_start is None:
                kk = kn_ref[rows, :]
                vv = v_ref[rows, :]
                bias = bias_ref[branch, :, BLOCK:]
            else:
                prev = rows_at(prev_start)
                kk = jnp.concatenate([kn_ref[prev, :], kn_ref[rows, :]], axis=0)
                vv = jnp.concatenate([v_ref[prev, :], v_ref[rows, :]], axis=0)
                bias = bias_ref[branch]
            s = _dot_nt(q2, kk.astype(BF16)) + bias
            m2 = jnp.broadcast_to(jnp.max(s, axis=-1, keepdims=True), (2 * BLOCK, PAIR))
            p = jnp.exp(s - jnp.concatenate([m2] * (s.shape[1] // PAIR), axis=1))
            v_ext = jnp.concatenate([vv.astype(BF16), jnp.ones(vv.shape, BF16)], axis=1)
            o_ext = _dot(p.astype(BF16), v_ext)
            num = _unstack_heads(o_ext[:, :PAIR])
            l_new = _unstack_heads(o_ext[:, PAIR:])
            m_new = _unstack_heads(m2)
            if branch == 0:
                m_ref[rows, :] = m_new
                l_ref[rows, :] = l_new
                acc_ref[rows, :] = num
            else:
                m_old = m_ref[rows, :]
                m_tot = jnp.maximum(m_old, m_new)
                w_old = jnp.exp(m_old - m_tot)
                w_new = jnp.exp(m_new - m_tot)
                m_ref[rows, :] = m_tot
                l_ref[rows, :] = l_ref[rows, :] * w_old + l_new * w_new
                acc_ref[rows, :] = acc_ref[rows, :] * w_old + num * w_new

        span = BLOCK * dil
        if dil == 1:
            block(0, None)
            for u in range(1, ATTN_UNROLL):
                block(u * BLOCK, (u - 1) * BLOCK)

            def group(g, _, block=block):
                for u in range(ATTN_UNROLL):
                    start = (g * ATTN_UNROLL + u) * BLOCK
                    block(start, start - BLOCK)
                return 0

            lax.fori_loop(1, n_sub // ATTN_UNROLL, group, 0)
        else:
            def residues(rg, _, block=block, span=span, n_sub=n_sub):
                first = rg * ATTN_UNROLL
                for u in range(ATTN_UNROLL):
                    block(first + u, None)

                def later(j, _):
                    for u in range(ATTN_UNROLL):
                        block(j * span + first + u, (j - 1) * span + first + u)
                    return 0

                if n_sub <= 3:
                    for j in range(1, n_sub):
                        later(j, 0)
                else:
                    lax.fori_loop(1, n_sub, later, 0)
                return 0

            if dil == ATTN_UNROLL:
                residues(0, 0)
            else:
                lax.fori_loop(0, dil // ATTN_UNROLL, residues, 0)

    def finish(c, _):
        rows = pl.ds(pl.multiple_of(c * PRE_CHUNK, PRE_CHUNK), PRE_CHUNK)
        o_ref[rows, :] = acc_ref[rows, :] / l_ref[rows, :]
        return 0

    lax.fori_loop(0, seq // PRE_CHUNK, finish, 0)


def _attn_c(proj, gq2, gk2, bias, layer):
    b, s, _ = proj.shape
    n_pairs = N_HEADS_C // 2
    n_br = len(DILATED_PAIRS)
    col = lambda off: pl.BlockSpec((None, s, PAIR), lambda bi, p: (bi, 0, off + p))
    return pl.pallas_call(
        _attn_c_kernel,
        grid=(b, n_pairs),
        in_specs=[
            col(COL_CQ), col(COL_CK), col(COL_CV),
            pl.BlockSpec((None, 1, PAIR), lambda bi, p: (layer, 0, 0)),
            pl.BlockSpec((None, 1, PAIR), lambda bi, p: (layer, 0, 0)),
            pl.BlockSpec((n_br, None, 2 * BLOCK, 2 * BLOCK), lambda bi, p: (0, p, 0, 0)),
        ],
        out_specs=pl.BlockSpec((None, s, PAIR), lambda bi, p: (bi, 0, p)),
        out_shape=jax.ShapeDtypeStruct((b, s, C_W), F32),
        scratch_shapes=[pltpu.VMEM((s, PAIR), F32) for _ in range(5)],
        compiler_params=_params(("arbitrary", "arbitrary")),
        name="attn_dilated",
    )(proj, proj, proj, gq2, gk2, bias)


def _out_proj_kernel(a_ref, b_ref, c_ref, g_ref, w_ref, x_ref, o_ref):
    ya = _rms_rows(a_ref[...], g_ref[:, 0:A_Q]).astype(BF16)
    yb = _rms_rows(b_ref[...], g_ref[:, A_Q:A_Q + B_W]).astype(BF16)
    yc = _rms_rows(c_ref[...], g_ref[:, A_Q + B_W:]).astype(BF16)
    acc = x_ref[...] + _dot(ya, w_ref[0:A_Q, :])
    acc = acc + _dot(yb, w_ref[A_Q:A_Q + B_W, :])
    o_ref[...] = acc + _dot(yc, w_ref[A_Q + B_W:, :])


def _out_proj(out_a, out_b, out_c, g, w, x2, layer):
    m, d = x2.shape
    tm = ROW_TILE_OUT
    rows = lambda width: pl.BlockSpec((tm, width), lambda i: (i, 0))
    return pl.pallas_call(
        _out_proj_kernel,
        grid=(m // tm,),
        in_specs=[
            rows(A_Q), rows(B_W), rows(C_W),
            pl.BlockSpec((None, 1, d), lambda i: (layer, 0, 0)),
            pl.BlockSpec((None, d, d), lambda i: (layer, 0, 0)),
            rows(d),
        ],
        out_specs=rows(d),
        out_shape=jax.ShapeDtypeStruct((m, d), F32),
        compiler_params=_params(("arbitrary",)),
        name="out_proj",
    )(out_a, out_b, out_c, g, w, x2)


def _ffn_kernel(x_ref, g_ref, wg_ref, wu_ref, cwg_ref, cwu_ref, cbg_ref, cbu_ref, wd_ref, o_ref, h_ref,
                *, tiles_per_seq):
    i = pl.program_id(0)
    tm = x_ref.shape[0]

    @pl.when(pl.program_id(1) == 0)
    def _():
        @pl.when(i % tiles_per_seq == 0)
        def _():
            h_ref[0:HALO, :] = jnp.zeros((HALO, h_ref.shape[1]), BF16)

        @pl.when(i % tiles_per_seq != 0)
        def _():
            h_ref[0:HALO, :] = h_ref[tm:tm + HALO, :]

        x = x_ref[...]
        h_ref[HALO:, :] = _rms_rows(x, g_ref[...]).astype(BF16)
        o_ref[...] = x

    h = h_ref[...]

    def conv_branch(w_ref, cw_ref, cb_ref):
        u = _dot(h, w_ref[...])
        cw = cw_ref[...]
        acc = cb_ref[...] + cw[0:1, :] * pltpu.roll(u, 2, 0)
        acc = acc + cw[1:2, :] * pltpu.roll(u, 1, 0)
        acc = acc + cw[2:3, :] * u
        return acc[HALO:, :]

    gate = conv_branch(wg_ref, cwg_ref, cbg_ref)
    up = conv_branch(wu_ref, cwu_ref, cbu_ref)
    act = gate * (1.0 / (1.0 + jnp.exp(-gate))) * up
    o_ref[...] += _dot(act.astype(BF16), wd_ref[...])


def _ffn(x2, g, w_up, conv_w, conv_b, w_down, layer, seq):
    m, d = x2.shape
    tm, fc = ROW_TILE_FFN, FF_CHUNK
    n_chunks = D_FF // fc
    return pl.pallas_call(
        functools.partial(_ffn_kernel, tiles_per_seq=seq // tm),
        grid=(m // tm, n_chunks),
        in_specs=[
            pl.BlockSpec((tm, d), lambda i, j: (i, 0)),
            pl.BlockSpec((None, 1, d), lambda i, j: (layer, 0, 0)),
            pl.BlockSpec((None, d, fc), lambda i, j: (layer, 0, j)),
            pl.BlockSpec((None, d, fc), lambda i, j: (layer, 0, j + n_chunks)),
            pl.BlockSpec((None, 3, fc), lambda i, j: (layer, 0, j)),
            pl.BlockSpec((None, 3, fc), lambda i, j: (layer, 0, j + n_chunks)),
            pl.BlockSpec((None, 1, fc), lambda i, j: (layer, 0, j)),
            pl.BlockSpec((None, 1, fc), lambda i, j: (layer, 0, j + n_chunks)),
            pl.BlockSpec((None, fc, d), lambda i, j: (layer, j, 0)),
        ],
        out_specs=pl.BlockSpec((tm, d), lambda i, j: (i, 0)),
        out_shape=jax.ShapeDtypeStruct((m, d), F32),
        scratch_shapes=[pltpu.VMEM((HALO + tm, d), BF16)],
        compiler_params=_params(("arbitrary", "arbitrary")),
        name="conv_ffn",
    )(x2, g, w_up, w_up, conv_w, conv_w, conv_b, conv_b, w_down)


def _t5_bucket(dist):
    max_exact = N_BUCKETS // 2
    d = jnp.maximum(dist, 0)
    large = max_exact + (jnp.log(jnp.maximum(d, 1).astype(jnp.float32) / max_exact)
                         / math.log(T5_MAX_DIST / max_exact) * (N_BUCKETS - max_exact)).astype(jnp.int32)
    large = jnp.minimum(large, N_BUCKETS - 1)
    return jnp.where(d < max_exact, d, large)


def _masked_pair_bias(table, dil, max_dist):
    rel = jnp.arange(BLOCK)[:, None] + BLOCK - jnp.arange(2 * BLOCK)[None, :]
    bias = jnp.transpose(table[_t5_bucket(rel * dil)], (2, 0, 1)).astype(F32)
    bias = jnp.where(((rel >= 0) & (rel <= max_dist))[None], bias, NEG_INF)
    return bias.reshape(table.shape[1] // 2, 2 * BLOCK, 2 * BLOCK)


def kernel(x, attn_norm, w_in, a_q_gain, a_k_gain, a_sinks, c_q_gain, c_k_gain, rel_bias_table,
           mix_out_gain, w_out, ffn_norm, w_up, conv_w, conv_b, w_down):
    b, s, d = x.shape
    depth = w_in.shape[0]
    assert d == D_MODEL and s % (BLOCK * DILATED_PAIRS[-1][1]) == 0 and s % ROW_TILE_IN == 0

    w_in_b, w_out_b, w_up_b, w_down_b = (w.astype(BF16) for w in (w_in, w_out, w_up, w_down))
    row = lambda t: t.reshape(depth, 1, -1)
    twice = lambda t: jnp.concatenate([t, t], axis=-1).reshape(depth, 1, PAIR)
    attn_norm_r, ffn_norm_r, mix_gain_r, conv_b_r = row(attn_norm), row(ffn_norm), row(mix_out_gain), row(conv_b)
    aq2, ak2, cq2, ck2 = twice(a_q_gain), twice(a_k_gain), twice(c_q_gain), twice(c_k_gain)
    sinks = jnp.broadcast_to(jnp.repeat(a_sinks.reshape(depth, N_HEADS_A // 2, 2), BLOCK, axis=-1)[..., None],
                             (depth, N_HEADS_A // 2, 2 * BLOCK, PAIR))
    bias_a = _masked_pair_bias(rel_bias_table[:, :N_HEADS_A], 1, WINDOW_A - 1)
    bias_c = jnp.stack([_masked_pair_bias(rel_bias_table[:, N_HEADS_A:], dil, window // dil)
                        for window, dil in DILATED_PAIRS])

    x2 = x.reshape(b * s, d)
    for layer in range(depth):
        proj = _in_proj(x2, attn_norm_r, w_in_b, layer).reshape(b, s, IN_WIDTH)
        out_a = _attn_a(proj, aq2, ak2, sinks, bias_a, layer)
        out_b = _attn_b(proj)
        out_c = _attn_c(proj, cq2, ck2, bias_c, layer)
        x2 = _out_proj(out_a.reshape(b * s, A_Q), out_b.reshape(b * s, B_W), out_c.reshape(b * s, C_W),
                       mix_gain_r, w_out_b, x2, layer)
        x2 = _ffn(x2, ffn_norm_r, w_up_b, conv_w, conv_b_r, w_down_b, layer, s)
    return x2.reshape(b, s, d)
```

```python
import functools
import math

import jax
import jax.numpy as jnp
import numpy as np
from jax import lax
from jax.experimental import pallas as pl
from jax.experimental.pallas import tpu as pltpu

F32 = jnp.float32
BF16 = jnp.bfloat16

D_MODEL = 2048
HEAD_DIM = 64
PAIR = 2 * HEAD_DIM
N_HEADS_A = 8
N_KV_A = 2
N_HEADS_B = 8
N_HEADS_C = 16
BLOCK = 128
WINDOW_A = 128
DILATED_PAIRS = ((128, 1), (512, 4), (2048, 16))
N_BUCKETS = 32
T5_MAX_DIST = 2048
D_FF = 5632
EPS = 1e-6
NEG_INF = -1e30

A_Q = N_HEADS_A * HEAD_DIM
A_KV = N_KV_A * HEAD_DIM
B_W = N_HEADS_B * HEAD_DIM
C_W = N_HEADS_C * HEAD_DIM
IN_WIDTH = A_Q + 2 * A_KV + 3 * B_W + 3 * C_W

COL_AQ = 0
COL_AK = A_Q // PAIR
COL_AV = COL_AK + A_KV // PAIR
COL_BQ = COL_AV + A_KV // PAIR
COL_BK = COL_BQ + B_W // PAIR
COL_BV = COL_BK + B_W // PAIR
COL_CQ = COL_BV + B_W // PAIR
COL_CK = COL_CQ + C_W // PAIR
COL_CV = COL_CK + C_W // PAIR

VMEM_LIMIT = 56 * 1024 * 1024

ROW_TILE_IN = 1024
COL_TILE_IN = 768
ROW_TILE_OUT = 512
ROW_TILE_FFN = 1024
FF_CHUNK = 512
HALO = 16
SB_QROWS = 512
SB_KEYS = 256
PRE_CHUNK = 512
ATTN_UNROLL = 4


def _params(semantics):
    return pltpu.CompilerParams(dimension_semantics=semantics, vmem_limit_bytes=VMEM_LIMIT)


def _rms_rows(x, g):
    ms = jnp.mean(x * x, axis=-1, keepdims=True)
    return x * lax.rsqrt(ms + EPS) * g


def _lane_is_first_head(rows=1):
    return lax.broadcasted_iota(jnp.int32, (rows, PAIR), 1) < HEAD_DIM


def _pair_rmsnorm(x, g2):
    first = _lane_is_first_head()
    sq = x * x
    s_first = jnp.sum(jnp.where(first, sq, 0.0), axis=-1, keepdims=True)
    s_second = jnp.sum(jnp.where(first, 0.0, sq), axis=-1, keepdims=True)
    ms = jnp.where(first, s_first, s_second) * (1.0 / HEAD_DIM)
    return x * lax.rsqrt(ms + EPS) * g2


def _stack_heads(qb):
    first = _lane_is_first_head()
    return jnp.concatenate([jnp.where(first, qb, 0.0), jnp.where(first, 0.0, qb)], axis=0)


def _unstack_heads(o2):
    rows = o2.shape[0] // 2
    return jnp.where(_lane_is_first_head(), o2[:rows], o2[rows:])


def _dot_nt(a, b):
    return lax.dot_general(a, b, (((1,), (1,)), ((), ())), preferred_element_type=F32)


def _dot(a, b):
    return jnp.dot(a, b, preferred_element_type=F32)


def _in_proj_kernel(x_ref, g_ref, w_ref, o_ref, h_ref):
    @pl.when(pl.program_id(1) == 0)
    def _():
        h_ref[...] = _rms_rows(x_ref[...], g_ref[...]).astype(BF16)

    o_ref[...] = _dot(h_ref[...], w_ref[...])


def _in_proj(x2, g, w, layer):
    m, d = x2.shape
    n = w.shape[-1]
    tm, tn = ROW_TILE_IN, COL_TILE_IN
    return pl.pallas_call(
        _in_proj_kernel,
        grid=(m // tm, n // tn),
        in_specs=[
            pl.BlockSpec((tm, d), lambda i, j: (i, 0)),
            pl.BlockSpec((None, 1, d), lambda i, j: (layer, 0, 0)),
            pl.BlockSpec((None, d, tn), lambda i, j: (layer, 0, j)),
        ],
        out_specs=pl.BlockSpec((tm, tn), lambda i, j: (i, j)),
        out_shape=jax.ShapeDtypeStruct((m, n), F32),
        scratch_shapes=[pltpu.VMEM((tm, d), BF16)],
        compiler_params=_params(("arbitrary", "arbitrary")),
        name="in_proj",
    )(x2, g, w)


def _attn_a_kernel(q_ref, k_ref, v_ref, gq_ref, gk_ref, sink_ref, bias_ref, o_ref, qn_ref, kn_ref, vn_ref):
    seq = q_ref.shape[0]
    kv_head = pl.program_id(1) // (N_HEADS_A // N_KV_A // 2)
    keep = jnp.logical_xor(_lane_is_first_head(), kv_head == 1)

    def prep(c, _):
        rows = pl.ds(pl.multiple_of(c * PRE_CHUNK, PRE_CHUNK), PRE_CHUNK)
        qn_ref[rows, :] = _pair_rmsnorm(q_ref[rows, :], gq_ref[...]) * (HEAD_DIM ** -0.5)
        kn = _pair_rmsnorm(k_ref[rows, :], gk_ref[...])
        kn_ref[rows, :] = jnp.where(keep, kn, pltpu.roll(kn, HEAD_DIM, 1)).astype(BF16)
        v = v_ref[rows, :]
        vn_ref[rows, 0:PAIR] = jnp.where(keep, v, pltpu.roll(v, HEAD_DIM, 1)).astype(BF16)
        vn_ref[rows, PAIR:] = jnp.ones((PRE_CHUNK, PAIR), BF16)
        return 0

    lax.fori_loop(0, seq // PRE_CHUNK, prep, 0)
    sink = sink_ref[...]

    def block(r0, kk, vv, bias):
        q2 = _stack_heads(qn_ref[pl.ds(r0, BLOCK), :]).astype(BF16)
        s = _dot_nt(q2, kk) + bias
        m = jnp.maximum(jnp.max(s, axis=-1, keepdims=True), sink)
        p = jnp.exp(s - jnp.concatenate([m] * (s.shape[1] // PAIR), axis=1))
        o_ext = _dot(p.astype(BF16), vv)
        den = o_ext[:, PAIR:] + jnp.exp(sink - m)
        o_ref[pl.ds(r0, BLOCK), :] = _unstack_heads(o_ext[:, :PAIR] / den)

    def later_block(r0):
        keys = pl.ds(r0 - BLOCK, 2 * BLOCK)
        block(r0, kn_ref[keys, :], vn_ref[keys, :], bias_ref[...])

    block(0, kn_ref[0:BLOCK, :], vn_ref[0:BLOCK, :], bias_ref[:, BLOCK:])
    for u in range(1, ATTN_UNROLL):
        later_block(u * BLOCK)

    def body(g, _):
        for u in range(ATTN_UNROLL):
            later_block(pl.multiple_of((g * ATTN_UNROLL + u) * BLOCK, BLOCK))
        return 0

    lax.fori_loop(1, seq // (BLOCK * ATTN_UNROLL), body, 0)


def _attn_a(proj, gq2, gk2, sinks, bias, layer):
    b, s, _ = proj.shape
    n_pairs = N_HEADS_A // 2
    col = lambda off: pl.BlockSpec((None, s, PAIR), lambda bi, p: (bi, 0, off + p))
    fixed = lambda off: pl.BlockSpec((None, s, PAIR), lambda bi, p: (bi, 0, off))
    return pl.pallas_call(
        _attn_a_kernel,
        grid=(b, n_pairs),
        in_specs=[
            col(COL_AQ), fixed(COL_AK), fixed(COL_AV),
            pl.BlockSpec((None, 1, PAIR), lambda bi, p: (layer, 0, 0)),
            pl.BlockSpec((None, 1, PAIR), lambda bi, p: (layer, 0, 0)),
            pl.BlockSpec((None, None, 2 * BLOCK, PAIR), lambda bi, p: (layer, p, 0, 0)),
            pl.BlockSpec((None, 2 * BLOCK, 2 * BLOCK), lambda bi, p: (p, 0, 0)),
        ],
        out_specs=pl.BlockSpec((None, s, PAIR), lambda bi, p: (bi, 0, p)),
        out_shape=jax.ShapeDtypeStruct((b, s, A_Q), F32),
        scratch_shapes=[pltpu.VMEM((s, PAIR), F32), pltpu.VMEM((s, PAIR), BF16), pltpu.VMEM((s, 2 * PAIR), BF16)],
        compiler_params=_params(("arbitrary", "arbitrary")),
        name="attn_swa",
    )(proj, proj, proj, gq2, gk2, sinks, bias)


LOG2_E = 1.0 / math.log(2.0)


SB_LOGIT_MAX = 126.0


def _softplus2(y):
    return jnp.log2(1.0 + jnp.exp2(y))


def _attn_b_kernel(q_ref, k_ref, v_ref, o_ref, q2_ref, kb_ref, vb_ref, mask_ref, z_ref, zm_ref, cum_ref, acc_ref,
                   run_ref):
    seq = q_ref.shape[0]
    qr, kt = SB_QROWS, SB_KEYS
    nq = seq // qr
    per_q = qr // kt
    n_tiles = per_q * nq * (nq + 1) // 2

    def prep(c, _):
        rows = pl.ds(pl.multiple_of(c * qr, qr), qr)
        kb_ref[rows, :] = k_ref[rows, :].astype(BF16)
        vb_ref[rows, :] = v_ref[rows, :].astype(BF16)
        q2_ref[c] = _stack_heads(q_ref[rows, :] * (LOG2_E * HEAD_DIM ** -0.5)).astype(BF16)
        return 0

    lax.fori_loop(0, nq, prep, 0)

    t_local = lax.broadcasted_iota(jnp.int32, (2 * qr, kt), 0) & (qr - 1)
    s_local = lax.broadcasted_iota(jnp.int32, (2 * qr, kt), 1)
    mask_ref[0] = jnp.zeros((2 * qr, kt), F32)
    for d in range(per_q):
        mask_ref[1 + d] = jnp.where(s_local + d * kt < t_local, 0.0, NEG_INF)

    tri = (lax.broadcasted_iota(jnp.int32, (kt, kt), 0) >= lax.broadcasted_iota(jnp.int32, (kt, kt), 1)).astype(BF16)
    tri2 = jnp.concatenate([tri, tri], axis=0)

    def keys_of(j):
        return pl.ds(pl.multiple_of(j * kt, kt), kt)

    def following(i, j):
        wrap = j == 0
        i_next = jnp.minimum(jnp.where(wrap, i + 1, i), nq - 1)
        return i_next, jnp.where(wrap, (i_next + 1) * per_q - 1, j - 1)

    def logits(i, j):
        z_ref[...] = _dot_nt(q2_ref[i], kb_ref[keys_of(j), :])

    def scores(i, j):
        which_mask = jnp.maximum(j - i * per_q + 1, 0)
        zm = jnp.minimum(z_ref[...] + mask_ref[which_mask], SB_LOGIT_MAX)
        sp = _softplus2(zm)
        cum_ref[...] = _dot(sp.astype(BF16), tri)
        zm_ref[...] = zm

    def weights(i, j):
        cs = cum_ref[...] + run_ref[...]
        a = jnp.exp2(zm_ref[...] - cs)
        acc_ref[...] += _dot(a.astype(BF16), vb_ref[keys_of(j), :])
        run_ref[...] = jnp.broadcast_to(cs[:, 0:1], run_ref.shape)

    acc_ref[...] = jnp.zeros_like(acc_ref)
    run_ref[...] = jnp.zeros_like(run_ref)
    first = (jnp.int32(0), jnp.int32(per_q - 1))
    logits(*first)
    scores(*first)
    logits(*following(*first))

    def step(n, tile):
        i, j = tile
        i1, j1 = following(i, j)
        i2, j2 = following(i1, j1)
        weights(i, j)
        scores(i1, j1)
        logits(i2, j2)

        @pl.when(j == 0)
        def _():
            o_ref[pl.ds(pl.multiple_of(i * qr, qr), qr), :] = _unstack_heads(acc_ref[...])
            acc_ref[...] = jnp.zeros_like(acc_ref)
            run_ref[...] = jnp.zeros_like(run_ref)

        return i1, j1

    lax.fori_loop(0, n_tiles, step, first)


def _attn_b(proj):
    b, s, _ = proj.shape
    n_pairs = N_HEADS_B // 2
    tile = (2 * SB_QROWS, SB_KEYS)
    col = lambda off: pl.BlockSpec((None, s, PAIR), lambda bi, p: (bi, 0, off + p))
    return pl.pallas_call(
        _attn_b_kernel,
        grid=(b, n_pairs),
        in_specs=[col(COL_BQ), col(COL_BK), col(COL_BV)],
        out_specs=pl.BlockSpec((None, s, PAIR), lambda bi, p: (bi, 0, p)),
        out_shape=jax.ShapeDtypeStruct((b, s, B_W), F32),
        scratch_shapes=[pltpu.VMEM((s // SB_QROWS, 2 * SB_QROWS, PAIR), BF16),
                        pltpu.VMEM((s, PAIR), BF16), pltpu.VMEM((s, PAIR), BF16),
                        pltpu.VMEM((1 + SB_QROWS // SB_KEYS,) + tile, F32),
                        pltpu.VMEM(tile, F32), pltpu.VMEM(tile, F32), pltpu.VMEM(tile, F32),
                        pltpu.VMEM((2 * SB_QROWS, PAIR), F32), pltpu.VMEM(tile, F32)],
        compiler_params=_params(("arbitrary", "arbitrary")),
        name="attn_stickbreak",
    )(proj, proj, proj)


def _attn_c_kernel(q_ref, k_ref, v_ref, gq_ref, gk_ref, bias_ref, o_ref, qn_ref, kn_ref, m_ref, l_ref, acc_ref):
    seq = q_ref.shape[0]

    def prep(c, _):
        rows = pl.ds(pl.multiple_of(c * PRE_CHUNK, PRE_CHUNK), PRE_CHUNK)
        qn_ref[rows, :] = _pair_rmsnorm(q_ref[rows, :], gq_ref[...]) * (HEAD_DIM ** -0.5)
        kn_ref[rows, :] = _pair_rmsnorm(k_ref[rows, :], gk_ref[...])
        return 0

    lax.fori_loop(0, seq // PRE_CHUNK, prep, 0)

    for branch, (_, dil) in enumerate(DILATED_PAIRS):
        n_sub = seq // (BLOCK * dil)

        def rows_at(start, dil=dil):
            if dil == 1:
                return pl.ds(start if isinstance(start, int) else pl.multiple_of(start, BLOCK), BLOCK)
            return pl.ds(start, BLOCK, stride=dil)

        def block(start, prev_start, branch=branch, rows_at=rows_at):
            rows = rows_at(start)
            q2 = _stack_heads(qn_ref[rows, :]).astype(BF16)
            if prev_start is None:
                kk = kn_ref[rows, :]
                vv = v_ref[rows, :]
                bias = bias_ref[branch, :, BLOCK:]
            else:
                prev = rows_at(prev_start)
                kk = jnp.concatenate([kn_ref[prev, :], kn_ref[rows, :]], axis=0)
                vv = jnp.concatenate([v_ref[prev, :], v_ref[rows, :]], axis=0)
                bias = bias_ref[branch]
            s = _dot_nt(q2, kk.astype(BF16)) + bias
            m2 = jnp.broadcast_to(jnp.max(s, axis=-1, keepdims=True), (2 * BLOCK, PAIR))
            p = jnp.exp(s - jnp.concatenate([m2] * (s.shape[1] // PAIR), axis=1))
            v_ext = jnp.concatenate([vv.astype(BF16), jnp.ones(vv.shape, BF16)], axis=1)
            o_ext = _dot(p.astype(BF16), v_ext)
            num = _unstack_heads(o_ext[:, :PAIR])
            l_new = _unstack_heads(o_ext[:, PAIR:])
            m_new = _unstack_heads(m2)
            if branch == 0:
                m_ref[rows, :] = m_new
                l_ref[rows, :] = l_new
                acc_ref[rows, :] = num
            else:
                m_old = m_ref[rows, :]
                m_tot = jnp.maximum(m_old, m_new)
                w_old = jnp.exp(m_old - m_tot)
                w_new = jnp.exp(m_new - m_tot)
                m_ref[rows, :] = m_tot
                l_ref[rows, :] = l_ref[rows, :] * w_old + l_new * w_new
                acc_ref[rows, :] = acc_ref[rows, :] * w_old + num * w_new

        span = BLOCK * dil
        if dil == 1:
            block(0, None)
            for u in range(1, ATTN_UNROLL):
                block(u * BLOCK, (u - 1) * BLOCK)

            def group(g, _, block=block):
                for u in range(ATTN_UNROLL):
                    start = (g * ATTN_UNROLL + u) * BLOCK
                    block(start, start - BLOCK)
                return 0

            lax.fori_loop(1, n_sub // ATTN_UNROLL, group, 0)
        else:
            def residues(rg, _, block=block, span=span, n_sub=n_sub):
                first = rg * ATTN_UNROLL
                for u in range(ATTN_UNROLL):
                    block(first + u, None)

                def later(j, _):
                    for u in range(ATTN_UNROLL):
                        block(j * span + first + u, (j - 1) * span + first + u)
                    return 0

                if n_sub <= 3:
                    for j in range(1, n_sub):
                        later(j, 0)
                else:
                    lax.fori_loop(1, n_sub, later, 0)
                return 0

            if dil == ATTN_UNROLL:
                residues(0, 0)
            else:
                lax.fori_loop(0, dil // ATTN_UNROLL, residues, 0)

    def finish(c, _):
        rows = pl.ds(pl.multiple_of(c * PRE_CHUNK, PRE_CHUNK), PRE_CHUNK)
        o_ref[rows, :] = acc_ref[rows, :] / l_ref[rows, :]
        return 0

    lax.fori_loop(0, seq // PRE_CHUNK, finish, 0)


def _attn_c(proj, gq2, gk2, bias, layer):
    b, s, _ = proj.shape
    n_pairs = N_HEADS_C // 2
    n_br = len(DILATED_PAIRS)
    col = lambda off: pl.BlockSpec((None, s, PAIR), lambda bi, p: (bi, 0, off + p))
    return pl.pallas_call(
        _attn_c_kernel,
        grid=(b, n_pairs),
        in_specs=[
            col(COL_CQ), col(COL_CK), col(COL_CV),
            pl.BlockSpec((None, 1, PAIR), lambda bi, p: (layer, 0, 0)),
            pl.BlockSpec((None, 1, PAIR), lambda bi, p: (layer, 0, 0)),
            pl.BlockSpec((n_br, None, 2 * BLOCK, 2 * BLOCK), lambda bi, p: (0, p, 0, 0)),
        ],
        out_specs=pl.BlockSpec((None, s, PAIR), lambda bi, p: (bi, 0, p)),
        out_shape=jax.ShapeDtypeStruct((b, s, C_W), F32),
        scratch_shapes=[pltpu.VMEM((s, PAIR), F32) for _ in range(5)],
        compiler_params=_params(("arbitrary", "arbitrary")),
        name="attn_dilated",
    )(proj, proj, proj, gq2, gk2, bias)


def _out_proj_kernel(a_ref, b_ref, c_ref, g_ref, w_ref, x_ref, o_ref):
    ya = _rms_rows(a_ref[...], g_ref[:, 0:A_Q]).astype(BF16)
    yb = _rms_rows(b_ref[...], g_ref[:, A_Q:A_Q + B_W]).astype(BF16)
    yc = _rms_rows(c_ref[...], g_ref[:, A_Q + B_W:]).astype(BF16)
    acc = x_ref[...] + _dot(ya, w_ref[0:A_Q, :])
    acc = acc + _dot(yb, w_ref[A_Q:A_Q + B_W, :])
    o_ref[...] = acc + _dot(yc, w_ref[A_Q + B_W:, :])


def _out_proj(out_a, out_b, out_c, g, w, x2, layer):
    m, d = x2.shape
    tm = ROW_TILE_OUT
    rows = lambda width: pl.BlockSpec((tm, width), lambda i: (i, 0))
    return pl.pallas_call(
        _out_proj_kernel,
        grid=(m // tm,),
        in_specs=[
            rows(A_Q), rows(B_W), rows(C_W),
            pl.BlockSpec((None, 1, d), lambda i: (layer, 0, 0)),
            pl.BlockSpec((None, d, d), lambda i: (layer, 0, 0)),
            rows(d),
        ],
        out_specs=rows(d),
        out_shape=jax.ShapeDtypeStruct((m, d), F32),
        compiler_params=_params(("arbitrary",)),
        name="out_proj",
    )(out_a, out_b, out_c, g, w, x2)


def _ffn_kernel(x_ref, g_ref, wg_ref, wu_ref, cwg_ref, cwu_ref, cbg_ref, cbu_ref, wd_ref, o_ref, h_ref, act_ref,
                *, tiles_per_seq, n_chunks):
    i = pl.program_id(0)
    j = pl.program_id(1)
    tm = x_ref.shape[0]

    @pl.when(j == 0)
    def _():
        @pl.when(i % tiles_per_seq == 0)
        def _():
            h_ref[0:HALO, :] = jnp.zeros((HALO, h_ref.shape[1]), BF16)

        @pl.when(i % tiles_per_seq != 0)
        def _():
            h_ref[0:HALO, :] = h_ref[tm:tm + HALO, :]

        x = x_ref[...]
        h_ref[HALO:, :] = _rms_rows(x, g_ref[...]).astype(BF16)
        o_ref[...] = x

    def conv_branch(w_ref, cw_ref, cb_ref):
        u = _dot(h_ref[...], w_ref[...])
        cw = cw_ref[...]
        acc = cb_ref[...] + cw[0:1, :] * pltpu.roll(u, 2, 0)
        acc = acc + cw[1:2, :] * pltpu.roll(u, 1, 0)
        acc = acc + cw[2:3, :] * u
        return acc[HALO:, :]

    def up_stage():
        gate = conv_branch(wg_ref, cwg_ref, cbg_ref)
        up = conv_branch(wu_ref, cwu_ref, cbu_ref)
        act_ref[...] = (gate * (1.0 / (1.0 + jnp.exp(-gate))) * up).astype(BF16)

    def down_stage():
        o_ref[...] += _dot(act_ref[...], wd_ref[...])

    @pl.when(j == 0)
    def _():
        up_stage()

    @pl.when(jnp.logical_and(j > 0, j < n_chunks))
    def _():
        down_stage()
        up_stage()

    @pl.when(j == n_chunks)
    def _():
        down_stage()


def _ffn(x2, g, w_up, conv_w, conv_b, w_down, layer, seq):
    m, d = x2.shape
    tm, fc = ROW_TILE_FFN, FF_CHUNK
    n_chunks = D_FF // fc
    up_chunk = lambda j: jnp.minimum(j, n_chunks - 1)
    down_chunk = lambda j: jnp.maximum(j - 1, 0)
    return pl.pallas_call(
        functools.partial(_ffn_kernel, tiles_per_seq=seq // tm, n_chunks=n_chunks),
        grid=(m // tm, n_chunks + 1),
        in_specs=[
            pl.BlockSpec((tm, d), lambda i, j: (i, 0), pipeline_mode=pl.Buffered(1)),
            pl.BlockSpec((None, 1, d), lambda i, j: (layer, 0, 0)),
            pl.BlockSpec((None, d, fc), lambda i, j: (layer, 0, up_chunk(j))),
            pl.BlockSpec((None, d, fc), lambda i, j: (layer, 0, up_chunk(j) + n_chunks)),
            pl.BlockSpec((None, 3, fc), lambda i, j: (layer, 0, up_chunk(j))),
            pl.BlockSpec((None, 3, fc), lambda i, j: (layer, 0, up_chunk(j) + n_chunks)),
            pl.BlockSpec((None, 1, fc), lambda i, j: (layer, 0, up_chunk(j))),
            pl.BlockSpec((None, 1, fc), lambda i, j: (layer, 0, up_chunk(j) + n_chunks)),
            pl.BlockSpec((None, fc, d), lambda i, j: (layer, down_chunk(j), 0)),
        ],
        out_specs=pl.BlockSpec((tm, d), lambda i, j: (i, 0)),
        out_shape=jax.ShapeDtypeStruct((m, d), F32),
        scratch_shapes=[pltpu.VMEM((HALO + tm, d), BF16), pltpu.VMEM((tm, fc), BF16)],
        compiler_params=_params(("arbitrary", "arbitrary")),
        name="conv_ffn",
    )(x2, g, w_up, w_up, conv_w, conv_w, conv_b, conv_b, w_down)


def _t5_bucket(dist):
    max_exact = N_BUCKETS // 2
    d = jnp.maximum(dist, 0)
    large = max_exact + (jnp.log(jnp.maximum(d, 1).astype(jnp.float32) / max_exact)
                         / math.log(T5_MAX_DIST / max_exact) * (N_BUCKETS - max_exact)).astype(jnp.int32)
    large = jnp.minimum(large, N_BUCKETS - 1)
    return jnp.where(d < max_exact, d, large)


def _masked_pair_bias(table, dil, max_dist):
    rel = jnp.arange(BLOCK)[:, None] + BLOCK - jnp.arange(2 * BLOCK)[None, :]
    bias = jnp.transpose(table[_t5_bucket(rel * dil)], (2, 0, 1)).astype(F32)
    bias = jnp.where(((rel >= 0) & (rel <= max_dist))[None], bias, NEG_INF)
    return bias.reshape(table.shape[1] // 2, 2 * BLOCK, 2 * BLOCK)


def kernel(x, attn_norm, w_in, a_q_gain, a_k_gain, a_sinks, c_q_gain, c_k_gain, rel_bias_table,
           mix_out_gain, w_out, ffn_norm, w_up, conv_w, conv_b, w_down):
    b, s, d = x.shape
    depth = w_in.shape[0]
    assert d == D_MODEL and s % (BLOCK * DILATED_PAIRS[-1][1]) == 0 and s % ROW_TILE_IN == 0

    w_in_b, w_out_b, w_up_b, w_down_b = (w.astype(BF16) for w in (w_in, w_out, w_up, w_down))
    row = lambda t: t.reshape(depth, 1, -1)
    twice = lambda t: jnp.concatenate([t, t], axis=-1).reshape(depth, 1, PAIR)
    attn_norm_r, ffn_norm_r, mix_gain_r, conv_b_r = row(attn_norm), row(ffn_norm), row(mix_out_gain), row(conv_b)
    aq2, ak2, cq2, ck2 = twice(a_q_gain), twice(a_k_gain), twice(c_q_gain), twice(c_k_gain)
    sinks = jnp.broadcast_to(jnp.repeat(a_sinks.reshape(depth, N_HEADS_A // 2, 2), BLOCK, axis=-1)[..., None],
                             (depth, N_HEADS_A // 2, 2 * BLOCK, PAIR))
    bias_a = _masked_pair_bias(rel_bias_table[:, :N_HEADS_A], 1, WINDOW_A - 1)
    bias_c = jnp.stack([_masked_pair_bias(rel_bias_table[:, N_HEADS_A:], dil, window // dil)
                        for window, dil in DILATED_PAIRS])

    x2 = x.reshape(b * s, d)
    for layer in range(depth):
        proj = _in_proj(x2, attn_norm_r, w_in_b, layer).reshape(b, s, IN_WIDTH)
        out_a = _attn_a(proj, aq2, ak2, sinks, bias_a, layer)
        out_b = _attn_b(proj)
        out_c = _attn_c(proj, cq2, ck2, bias_c, layer)
        x2 = _out_proj(out_a.reshape(b * s, A_Q), out_b.reshape(b * s, B_W), out_c.reshape(b * s, C_W),
                       mix_gain_r, w_out_b, x2, layer)
        x2 = _ffn(x2, ffn_norm_r, w_up_b, conv_w, conv_b_r, w_down_b, layer, s)
    return x2.reshape(b, s, d)
```
